```python
import jax, jax.numpy as jnp
from jax import lax
import numpy as np

D_MODEL = 1024
BATCH = 16
SEQ = 2048
DEPTH = 4
DEC_BATCH = 16
DEC_SEQ = 4096
PAST_LEN = 128

ATTN_GROUPS = ((128, 1), (512, 4), (2048, 16))
N_GROUPS = 3
HEADS_PER_GROUP = 4
HEAD_DIM = 128
N_ATTN_HEADS = N_GROUPS * HEADS_PER_GROUP
ATTN_QKV_W = N_ATTN_HEADS * HEAD_DIM
ATTN_OUT_W = HEADS_PER_GROUP * HEAD_DIM
D_RNN = 1280
RG_BLOCKS = 10
RG_BLOCK = D_RNN // RG_BLOCKS
RG_C = 8.0
CONV_WIDTH = 4
CONV_PAD = (2, 1)
D_FF = 2816
NORM_EPS = 1e-6
MASK_VALUE = -1e30
SPLIT_POINTS = (ATTN_QKV_W, 2 * ATTN_QKV_W, 3 * ATTN_QKV_W, 3 * ATTN_QKV_W + D_RNN, 3 * ATTN_QKV_W + 2 * D_RNN, 3 * ATTN_QKV_W + 2 * D_RNN + D_MODEL)
IN_COLS = 3 * ATTN_QKV_W + 2 * D_RNN + 2 * D_MODEL

kernel_name = 'hybrid_dilated_attn_rglru_encoder'


def rmsnorm(x, g):
    xf = x.astype(jnp.float32)
    y = xf * lax.rsqrt(jnp.mean(xf * xf, axis=-1, keepdims=True) + NORM_EPS)
    return (y * g.astype(jnp.float32)).astype(x.dtype)


def swiglu(x, w_gate, w_up, w_down):
    return (jax.nn.silu(x @ w_gate) * (x @ w_up)) @ w_down


def alibi_slopes():
    h = jnp.arange(1, N_ATTN_HEADS + 1, dtype=jnp.float32)
    return (2.0 ** (-8.0 * h / N_ATTN_HEADS)).reshape(N_GROUPS, HEADS_PER_GROUP)


def dilated_window_attention(q, k, v, window, dilation, slopes):
    B, S, H, Dh = q.shape
    n_side = (window // 2) // dilation
    blk = n_side
    L = S // dilation
    nb = -(-L // blk)
    Lp = nb * blk

    def to_sub(t):
        return t.reshape(B, L, dilation, H, Dh).transpose(0, 2, 1, 3, 4)

    qs = jnp.pad(to_sub(q), ((0, 0), (0, 0), (0, Lp - L), (0, 0), (0, 0)))
    qs = qs.reshape(B, dilation, nb, blk, H, Dh)
    pad_kv = ((0, 0), (0, 0), (blk, Lp - L + blk), (0, 0), (0, 0))

    def banded(t):
        tb = jnp.pad(to_sub(t), pad_kv).reshape(B, dilation, nb + 2, blk, H, Dh)
        return jnp.concatenate([tb[:, :, 0:nb], tb[:, :, 1:nb + 1], tb[:, :, 2:nb + 2]], axis=3)

    kb, vb = banded(k), banded(v)
    s = jnp.einsum('brnqhd,brnkhd->brhnqk', qs, kb, preferred_element_type=jnp.float32) * (Dh ** -0.5)
    rel = jnp.arange(3 * blk)[None, :] - blk - jnp.arange(blk)[:, None]
    key_pos = jnp.arange(nb)[:, None] * blk + jnp.arange(3 * blk)[None, :] - blk
    valid = (jnp.abs(rel) <= n_side)[None, :, :] & ((key_pos >= 0) & (key_pos < L))[:, None, :]
    alibi = -slopes.astype(jnp.float32)[:, None, None] * (dilation * jnp.abs(rel)).astype(jnp.float32)
    s = jnp.where(valid, s + alibi[:, None], MASK_VALUE)
    m = jnp.max(s, axis=-1, keepdims=True)
    p = jnp.exp(s - m)
    den = jnp.sum(p, axis=-1, keepdims=True)
    o = jnp.einsum('brhnqk,brnkhd->brhnqd', p.astype(vb.dtype), vb, preferred_element_type=jnp.float32) / den
    lse = (m + jnp.log(den))[..., 0]
    o = o.reshape(B, dilation, H, Lp, Dh)[:, :, :, :L].transpose(0, 3, 1, 2, 4).reshape(B, S, H, Dh)
    lse = lse.reshape(B, dilation, H, Lp)[..., :L].transpose(0, 3, 1, 2).reshape(B, S, H)
    return o, lse


def rglru_direction(x, w_a, b_a, w_x, b_x, lam, reverse):
    B, S, _ = x.shape
    xb = x.reshape(B, S, RG_BLOCKS, RG_BLOCK)
    r = jax.nn.sigmoid((jnp.einsum('bsnc,ncd->bsnd', xb, w_a).reshape(B, S, D_RNN) + b_a).astype(jnp.float32))
    i = jax.nn.sigmoid((jnp.einsum('bsnc,ncd->bsnd', xb, w_x).reshape(B, S, D_RNN) + b_x).astype(jnp.float32))
    log_a = -RG_C * r * jax.nn.softplus(-lam.astype(jnp.float32))
    a = jnp.exp(log_a)
    u = jnp.sqrt(-jnp.expm1(2.0 * log_a)) * (i * x.astype(jnp.float32))

    def step(h, au):
        a_t, u_t = au
        h = a_t * h + u_t
        return h, h

    h0 = jnp.zeros((B, D_RNN), jnp.float32)
    _, hs = lax.scan(step, h0, (a.transpose(1, 0, 2), u.transpose(1, 0, 2)), reverse=reverse)
    return hs.transpose(1, 0, 2)


def hybrid_mixer(xn, w_in, b_in, conv_w, conv_b, rg_w_a, rg_b_a, rg_w_x, rg_b_x, rg_lambda, w_proj_a, w_proj_b, w_out):
    B, S, _ = xn.shape
    z = xn @ w_in + b_in
    q, k, v, xr, gr, ga, gb = jnp.split(z, SPLIT_POINTS, axis=-1)

    q = q.reshape(B, S, N_GROUPS, HEADS_PER_GROUP, HEAD_DIM)
    k = k.reshape(B, S, N_GROUPS, HEADS_PER_GROUP, HEAD_DIM)
    v = v.reshape(B, S, N_GROUPS, HEADS_PER_GROUP, HEAD_DIM)
    slopes = alibi_slopes()
    outs, lses = [], []
    for g, (window, dilation) in enumerate(ATTN_GROUPS):
        o_g, lse_g = dilated_window_attention(q[:, :, g], k[:, :, g], v[:, :, g], window, dilation, slopes[g])
        outs.append(o_g)
        lses.append(lse_g)
    wts = jax.nn.softmax(jnp.stack(lses, axis=0), axis=0)
    o_att = jnp.sum(wts[..., None] * jnp.stack(outs, axis=0), axis=0)
    y_a = o_att.reshape(B, S, ATTN_OUT_W).astype(xn.dtype)

    xc = lax.conv_general_dilated(xr, conv_w[:, None, :], window_strides=(1,), padding=[CONV_PAD],
                                  dimension_numbers=('NWC', 'WIO', 'NWC'), feature_group_count=D_RNN) + conv_b
    h = (rglru_direction(xc, rg_w_a[0], rg_b_a[0], rg_w_x[0], rg_b_x[0], rg_lambda[0], False)
         + rglru_direction(xc, rg_w_a[1], rg_b_a[1], rg_w_x[1], rg_b_x[1], rg_lambda[1], True))
    y_b = (h * jax.nn.gelu(gr.astype(jnp.float32))).astype(xn.dtype)

    merged = jax.nn.sigmoid(ga) * (y_a @ w_proj_a) + jax.nn.sigmoid(gb) * (y_b @ w_proj_b)
    return merged @ w_out


def encoder_trunk(x, params):
    (ffn1_norm, ffn1_w_gate, ffn1_w_up, ffn1_w_down, mix_norm, w_in, b_in, conv_w, conv_b,
     rg_w_a, rg_b_a, rg_w_x, rg_b_x, rg_lambda, w_proj_a, w_proj_b, w_out,
     ffn2_norm, ffn2_w_gate, ffn2_w_up, ffn2_w_down, final_norm) = params
    for l in range(DEPTH):
        x = x + 0.5 * swiglu(rmsnorm(x, ffn1_norm[l]), ffn1_w_gate[l], ffn1_w_up[l], ffn1_w_down[l])
        x = x + hybrid_mixer(rmsnorm(x, mix_norm[l]), w_in[l], b_in[l], conv_w[l], conv_b[l],
                             rg_w_a[l], rg_b_a[l], rg_w_x[l], rg_b_x[l], rg_lambda[l],
                             w_proj_a[l], w_proj_b[l], w_out[l])
        x = x + 0.5 * swiglu(rmsnorm(x, ffn2_norm[l]), ffn2_w_gate[l], ffn2_w_up[l], ffn2_w_down[l])
    return rmsnorm(x, final_norm)


def _normal(key, shape, scale):
    return jax.random.normal(key, shape, jnp.float32) * scale


def setup_inputs(seed: int = 0) -> dict:
    key = jax.random.key(seed)
    ks = jax.random.split(key, 24)
    u = jax.random.uniform(ks[15], (DEPTH, 2, D_RNN), jnp.float32, 0.9, 0.999)
    a0 = u ** (1.0 / RG_C)
    return {
        'x_prompt': _normal(ks[0], (BATCH, SEQ, D_MODEL), 1.0),
        'x_sample': _normal(ks[1], (DEC_BATCH, DEC_SEQ, D_MODEL), 1.0),
        'ffn1_norm': 1.0 + _normal(ks[2], (DEPTH, D_MODEL), 0.02),
        'ffn1_w_gate': _normal(ks[3], (DEPTH, D_MODEL, D_FF), D_MODEL ** -0.5),
        'ffn1_w_up': _normal(ks[4], (DEPTH, D_MODEL, D_FF), D_MODEL ** -0.5),
        'ffn1_w_down': _normal(ks[5], (DEPTH, D_FF, D_MODEL), D_FF ** -0.5),
        'mix_norm': 1.0 + _normal(ks[6], (DEPTH, D_MODEL), 0.02),
        'w_in': _normal(ks[7], (DEPTH, D_MODEL, IN_COLS), D_MODEL ** -0.5),
        'b_in': _normal(ks[8], (DEPTH, IN_COLS), 0.02),
        'conv_w': _normal(ks[9], (DEPTH, CONV_WIDTH, D_RNN), CONV_WIDTH ** -0.5),
        'conv_b': _normal(ks[10], (DEPTH, D_RNN), 0.02),
        'rg_w_a': _normal(ks[11], (DEPTH, 2, RG_BLOCKS, RG_BLOCK, RG_BLOCK), RG_BLOCK ** -0.5),
        'rg_b_a': _normal(ks[12], (DEPTH, 2, D_RNN), 0.02),
        'rg_w_x': _normal(ks[13], (DEPTH, 2, RG_BLOCKS, RG_BLOCK, RG_BLOCK), RG_BLOCK ** -0.5),
        'rg_b_x': _normal(ks[14], (DEPTH, 2, D_RNN), 0.02),
        'rg_lambda': jnp.log(a0) - jnp.log1p(-a0),
        'w_proj_a': _normal(ks[16], (DEPTH, ATTN_OUT_W, D_MODEL), ATTN_OUT_W ** -0.5),
        'w_proj_b': _normal(ks[17], (DEPTH, D_RNN, D_MODEL), D_RNN ** -0.5),
        'w_out': _normal(ks[18], (DEPTH, D_MODEL, D_MODEL), D_MODEL ** -0.5),
        'ffn2_norm': 1.0 + _normal(ks[19], (DEPTH, D_MODEL), 0.02),
        'ffn2_w_gate': _normal(ks[20], (DEPTH, D_MODEL, D_FF), D_MODEL ** -0.5),
        'ffn2_w_up': _normal(ks[21], (DEPTH, D_MODEL, D_FF), D_MODEL ** -0.5),
        'ffn2_w_down': _normal(ks[22], (DEPTH, D_FF, D_MODEL), D_FF ** -0.5),
        'final_norm': 1.0 + _normal(ks[23], (D_MODEL,), 0.02),
    }


def reference(x_prompt, x_sample, ffn1_norm, ffn1_w_gate, ffn1_w_up, ffn1_w_down, mix_norm, w_in, b_in,
              conv_w, conv_b, rg_w_a, rg_b_a, rg_w_x, rg_b_x, rg_lambda, w_proj_a, w_proj_b, w_out,
              ffn2_norm, ffn2_w_gate, ffn2_w_up, ffn2_w_down, final_norm):
    params = (ffn1_norm, ffn1_w_gate, ffn1_w_up, ffn1_w_down, mix_norm, w_in, b_in, conv_w, conv_b,
              rg_w_a, rg_b_a, rg_w_x, rg_b_x, rg_lambda, w_proj_a, w_proj_b, w_out,
              ffn2_norm, ffn2_w_gate, ffn2_w_up, ffn2_w_down, final_norm)
    y_prompt = encoder_trunk(x_prompt, params)
    y_sample = encoder_trunk(x_sample, params)
    return (y_prompt, y_sample)
```

```python
import functools

import jax
import jax.numpy as jnp
from jax import lax
from jax.experimental import pallas as pl
from jax.experimental.pallas import tpu as pltpu

F32 = jnp.float32
BF16 = jnp.bfloat16

D_MODEL = 1024
DEPTH = 4
ATTN_GROUPS = ((128, 1), (512, 4), (2048, 16))
N_GROUPS = 3
HEADS_PER_GROUP = 4
HEAD_DIM = 128
N_ATTN_HEADS = N_GROUPS * HEADS_PER_GROUP
ATTN_QKV_W = N_ATTN_HEADS * HEAD_DIM
ATTN_OUT_W = HEADS_PER_GROUP * HEAD_DIM
D_RNN = 1280
RG_BLOCK = 128
RG_BLOCKS = D_RNN // RG_BLOCK
RG_C = 8.0
D_FF = 2816
NORM_EPS = 1e-6
MASK_VALUE = -1e30
N_SIDE = 64

VMEM_LIMIT_BYTES = 56 * 1024 * 1024
SUBLANES = 8
LANES = 128


def _params(*semantics):
    return pltpu.CompilerParams(dimension_semantics=semantics, vmem_limit_bytes=VMEM_LIMIT_BYTES)


def _rmsnorm_rows(x, g):
    y = x * lax.rsqrt(jnp.mean(x * x, axis=-1, keepdims=True) + NORM_EPS)
    return y * g


def _ffn_kernel(x_ref, g_ref, wg_ref, wu_ref, wd_ref, fg_ref, o_ref, xn_ref, acc_ref, *, n_ff, final_norm):
    j = pl.program_id(1)

    @pl.when(j == 0)
    def _():
        xn_ref[...] = _rmsnorm_rows(x_ref[...], g_ref[...]).astype(BF16)

    xn = xn_ref[...]
    gate = jnp.dot(xn, wg_ref[...], preferred_element_type=F32)
    up = jnp.dot(xn, wu_ref[...], preferred_element_type=F32)
    hid = (gate * jax.nn.sigmoid(gate) * up).astype(BF16)
    down = jnp.dot(hid, wd_ref[...], preferred_element_type=F32)

    @pl.when(j == 0)
    def _():
        acc_ref[...] = down

    @pl.when(j > 0)
    def _():
        acc_ref[...] += down

    @pl.when(j == n_ff - 1)
    def _():
        y = x_ref[...] + 0.5 * acc_ref[...]
        if final_norm:
            y = _rmsnorm_rows(y, fg_ref[...])
        o_ref[...] = y


def _ffn(x, g, wg, wu, wd, fg, *, final_norm, tm=512, tf=1408):
    m = x.shape[0]
    n_ff = D_FF // tf
    return pl.pallas_call(
        functools.partial(_ffn_kernel, n_ff=n_ff, final_norm=final_norm),
        out_shape=jax.ShapeDtypeStruct((m, D_MODEL), F32),
        grid=(m // tm, n_ff),
        in_specs=[
            pl.BlockSpec((tm, D_MODEL), lambda i, j: (i, 0)),
            pl.BlockSpec((1, D_MODEL), lambda i, j: (0, 0)),
            pl.BlockSpec((D_MODEL, tf), lambda i, j: (0, j)),
            pl.BlockSpec((D_MODEL, tf), lambda i, j: (0, j)),
            pl.BlockSpec((tf, D_MODEL), lambda i, j: (j, 0)),
            pl.BlockSpec((1, D_MODEL), lambda i, j: (0, 0)),
        ],
        out_specs=pl.BlockSpec((tm, D_MODEL), lambda i, j: (i, 0)),
        scratch_shapes=[pltpu.VMEM((tm, D_MODEL), BF16), pltpu.VMEM((tm, D_MODEL), F32)],
        compiler_params=_params("parallel", "arbitrary"),
        name="ffn",
    )(x, g, wg, wu, wd, fg)


def _norm_proj_kernel(x_ref, g_ref, w_ref, b_ref, o_ref, xn_ref):
    @pl.when(pl.program_id(1) == 0)
    def _():
        xn_ref[...] = _rmsnorm_rows(x_ref[...], g_ref[...]).astype(BF16)

    o_ref[...] = jnp.dot(xn_ref[...], w_ref[...], preferred_element_type=F32) + b_ref[...]


def _norm_proj(x, g, w, b, *, tn, tm=512):
    m = x.shape[0]
    n = w.shape[1]
    return pl.pallas_call(
        _norm_proj_kernel,
        out_shape=jax.ShapeDtypeStruct((m, n), F32),
        grid=(m // tm, n // tn),
        in_specs=[
            pl.BlockSpec((tm, D_MODEL), lambda i, j: (i, 0)),
            pl.BlockSpec((1, D_MODEL), lambda i, j: (0, 0)),
            pl.BlockSpec((D_MODEL, tn), lambda i, j: (0, j)),
            pl.BlockSpec((1, tn), lambda i, j: (0, j)),
        ],
        out_specs=pl.BlockSpec((tm, tn), lambda i, j: (i, j)),
        scratch_shapes=[pltpu.VMEM((tm, D_MODEL), BF16)],
        compiler_params=_params("parallel", "arbitrary"),
        name="norm_proj",
    )(x, g, w, b)


ATTN_BQ = 128
ATTN_BK = 256


def _attn_kernel(slopes_ref, q_ref, k_ref, v_ref, o_ref, m_ref, l_ref, acc_ref, *, seq):
    head = pl.program_id(1)
    grp = pl.program_id(2)

    def run_group(g, dilation):
        sub_len = seq // dilation
        bq = min(ATTN_BQ, sub_len)
        bk = min(ATTN_BK, sub_len)
        blocks_per_res = sub_len // bq
        n_blocks = dilation * blocks_per_res
        slope = slopes_ref[g, head] * float(dilation)
        row = lax.broadcasted_iota(jnp.int32, (bq, bk), 0)
        col = lax.broadcasted_iota(jnp.int32, (bq, bk), 1)
        col_minus_row = col - row

        def block(i, carry):
            res = i // blocks_per_res
            m0 = (i % blocks_per_res) * bq
            k0 = jnp.clip(m0 - N_SIDE, 0, sub_len - bk)
            q_rows = pl.ds(res + m0 * dilation, bq, stride=dilation)
            k_rows = pl.ds(res + k0 * dilation, bk, stride=dilation)
            q = (q_ref[q_rows, :] * (HEAD_DIM ** -0.5)).astype(BF16)
            k = k_ref[k_rows, :].astype(BF16)
            v = v_ref[k_rows, :].astype(BF16)
            s = lax.dot_general(q, k, (((1,), (1,)), ((), ())), preferred_element_type=F32)
            dist = jnp.abs(col_minus_row + (k0 - m0))
            s = jnp.where(dist <= N_SIDE, s - slope * dist.astype(F32), MASK_VALUE)
            m_blk = jnp.max(s, axis=-1, keepdims=True)
            if g == 0:
                m_new = m_blk
            else:
                m_old = m_ref[q_rows, :]
                m_new = jnp.maximum(m_old[:, :1], m_blk)
            p = jnp.exp(s - m_new)
            l_blk = jnp.sum(p, axis=-1, keepdims=True)
            pv = jnp.dot(p.astype(BF16), v, preferred_element_type=F32)
            if g == 0:
                l_new = jnp.broadcast_to(l_blk, (bq, HEAD_DIM))
                acc_new = pv
            else:
                alpha = jnp.exp(m_old - m_new)
                l_new = alpha * l_ref[q_rows, :] + l_blk
                acc_new = alpha * acc_ref[q_rows, :] + pv
            m_ref[q_rows, :] = jnp.broadcast_to(m_new, (bq, HEAD_DIM))
            l_ref[q_rows, :] = l_new
            acc_ref[q_rows, :] = acc_new
            return carry

        lax.fori_loop(0, n_blocks, block, 0)

    for g, (_, dilation) in enumerate(ATTN_GROUPS):
        pl.when(grp == g)(functools.partial(run_group, g, dilation))

    @pl.when(grp == N_GROUPS - 1)
    def _():
        def finish(i, carry):
            rows = pl.ds(pl.multiple_of(i * ATTN_BQ, ATTN_BQ), ATTN_BQ)
            o_ref[rows, :] = (acc_ref[rows, :] / l_ref[rows, :]).astype(o_ref.dtype)
            return carry

        lax.fori_loop(0, seq // ATTN_BQ, finish, 0)


def _attention(qkv, slopes, *, batch, seq):
    def col(part):
        return lambda b, h, g: (b, 0, part * N_ATTN_HEADS + g * HEADS_PER_GROUP + h)

    return pl.pallas_call(
        functools.partial(_attn_kernel, seq=seq),
        out_shape=jax.ShapeDtypeStruct((batch, seq, ATTN_OUT_W), BF16),
        grid=(batch, HEADS_PER_GROUP, N_GROUPS),
        in_specs=[
            pl.BlockSpec(memory_space=pltpu.SMEM),
            pl.BlockSpec((None, seq, HEAD_DIM), col(0)),
            pl.BlockSpec((None, seq, HEAD_DIM), col(1)),
            pl.BlockSpec((None, seq, HEAD_DIM), col(2)),
        ],
        out_specs=pl.BlockSpec((None, seq, HEAD_DIM), lambda b, h, g: (b, 0, h)),
        scratch_shapes=[
            pltpu.VMEM((seq, HEAD_DIM), F32),
            pltpu.VMEM((seq, HEAD_DIM), F32),
            pltpu.VMEM((seq, HEAD_DIM), F32),
        ],
        compiler_params=_params("parallel", "parallel", "arbitrary"),
        name="dilated_attention",
    )(slopes, qkv, qkv, qkv)


RG_TS = 32
CONV_LEFT = 2


def _rglru_kernel(xr_ref, prev_ref, next_ref, cw_ref, cb_ref, wg_ref, ba_ref, bx_ref, lam_ref,
                  h_ref, xt_ref, a_ref, u_ref, carry_ref, *, batch, n_chunks):
    ts = RG_TS
    direction = pl.program_id(0)
    step = pl.program_id(1)
    chunk = jnp.where(direction == 0, step, n_chunks - 1 - step)
    rows = ts * batch
    has_prev = chunk > 0
    has_next = chunk < n_chunks - 1

    @pl.when(step == 0)
    def _():
        carry_ref[...] = jnp.zeros_like(carry_ref)

    for n in range(RG_BLOCKS):
        cols = slice(n * RG_BLOCK, (n + 1) * RG_BLOCK)
        for b in range(batch):
            xt_ref[n, pl.ds(CONV_LEFT * batch + b, ts, stride=batch), :] = xr_ref[b, :, cols]
            halo = prev_ref[b, :, cols]
            for t in range(CONV_LEFT):
                r = t * batch + b
                src = SUBLANES - CONV_LEFT + t
                xt_ref[n, r:r + 1, :] = jnp.where(has_prev, halo[src:src + 1, :], 0.0)
            r = (CONV_LEFT + ts) * batch + b
            xt_ref[n, r:r + 1, :] = jnp.where(has_next, next_ref[b, 0:1, cols], 0.0)

        xc = cb_ref[:, cols]
        for tap in range(4):
            xc = xc + cw_ref[tap:tap + 1, cols] * xt_ref[n, tap * batch:tap * batch + rows, :]
        gates = jnp.dot(xc.astype(BF16), wg_ref[n], preferred_element_type=F32)
        r_gate = jax.nn.sigmoid(gates[:, :RG_BLOCK] + ba_ref[:, cols])
        i_gate = jax.nn.sigmoid(gates[:, RG_BLOCK:] + bx_ref[:, cols])
        log_a = (-RG_C) * r_gate * jax.nn.softplus(-lam_ref[:, cols])
        a = jnp.exp(log_a)
        a_ref[n] = a
        u_ref[n] = jnp.sqrt(-jnp.tanh(log_a) * (a * a + 1.0)) * (i_gate * xc)

    def scan_step(t, h):
        tt = jnp.where(direction == 0, t, ts - 1 - t)
        r = pl.ds(pl.multiple_of(tt * batch, batch), batch)
        h = a_ref[:, r, :] * h + u_ref[:, r, :]
        u_ref[:, r, :] = h
        return h

    carry_ref[...] = lax.fori_loop(0, ts, scan_step, carry_ref[...], unroll=4)

    for n in range(RG_BLOCKS):
        for b in range(batch):
            h_ref[b, :, n * RG_BLOCK:(n + 1) * RG_BLOCK] = u_ref[n, pl.ds(b, ts, stride=batch), :]


def _rglru(xg, conv_w, conv_b, w_gates, b_a, b_x, lam, *, batch, seq):
    ts = RG_TS
    n_chunks = seq // ts
    halo_per_chunk = ts // SUBLANES
    n_halo = seq // SUBLANES

    def chunk_of(d, s):
        return s + d * (n_chunks - 1 - 2 * s)

    return pl.pallas_call(
        functools.partial(_rglru_kernel, batch=batch, n_chunks=n_chunks),
        out_shape=jax.ShapeDtypeStruct((2, batch, seq, D_RNN), F32),
        grid=(2, n_chunks),
        in_specs=[
            pl.BlockSpec((batch, ts, D_RNN), lambda d, s: (0, chunk_of(d, s), 0)),
            pl.BlockSpec((batch, SUBLANES, D_RNN),
                         lambda d, s: (0, jnp.maximum(chunk_of(d, s) * halo_per_chunk - 1, 0), 0)),
            pl.BlockSpec((batch, SUBLANES, D_RNN),
                         lambda d, s: (0, jnp.minimum((chunk_of(d, s) + 1) * halo_per_chunk, n_halo - 1), 0)),
            pl.BlockSpec((4, D_RNN), lambda d, s: (0, 0)),
            pl.BlockSpec((1, D_RNN), lambda d, s: (0, 0)),
            pl.BlockSpec((None, RG_BLOCKS, RG_BLOCK, 2 * RG_BLOCK), lambda d, s: (d, 0, 0, 0)),
            pl.BlockSpec((None, 1, D_RNN), lambda d, s: (d, 0, 0)),
            pl.BlockSpec((None, 1, D_RNN), lambda d, s: (d, 0, 0)),
            pl.BlockSpec((None, 1, D_RNN), lambda d, s: (d, 0, 0)),
        ],
        out_specs=pl.BlockSpec((None, batch, ts, D_RNN), lambda d, s: (d, 0, chunk_of(d, s), 0)),
        scratch_shapes=[
            pltpu.VMEM((RG_BLOCKS, (ts + CONV_LEFT + 1) * batch, RG_BLOCK), F32),
            pltpu.VMEM((RG_BLOCKS, ts * batch, RG_BLOCK), F32),
            pltpu.VMEM((RG_BLOCKS, ts * batch, RG_BLOCK), F32),
            pltpu.VMEM((RG_BLOCKS, batch, RG_BLOCK), F32),
        ],
        compiler_params=_params("arbitrary", "arbitrary"),
        name="rglru_scan",
    )(xg, xg, xg, conv_w, conv_b, w_gates, b_a, b_x, lam)


def _merge_kernel(x_ref, ya_ref, hf_ref, hb_ref, gr_ref, ga_ref, gb_ref, wa_ref, wb_ref, wo_ref, o_ref):
    yb = ((hf_ref[...] + hb_ref[...]) * jax.nn.gelu(gr_ref[...])).astype(BF16)
    pa = jnp.dot(ya_ref[...], wa_ref[...], preferred_element_type=F32)
    pb = jnp.dot(yb, wb_ref[...], preferred_element_type=F32)
    merged = jax.nn.sigmoid(ga_ref[...]) * pa + jax.nn.sigmoid(gb_ref[...]) * pb
    o_ref[...] = x_ref[...] + jnp.dot(merged.astype(BF16), wo_ref[...], preferred_element_type=F32)


def _merge(x, ya, h, xg, gab, wa, wb, wo, *, tm=256):
    m = x.shape[0]
    const = lambda i: (0, 0)
    return pl.pallas_call(
        _merge_kernel,
        out_shape=jax.ShapeDtypeStruct((m, D_MODEL), F32),
        grid=(m // tm,),
        in_specs=[
            pl.BlockSpec((tm, D_MODEL), lambda i: (i, 0)),
            pl.BlockSpec((tm, ATTN_OUT_W), lambda i: (i, 0)),
            pl.BlockSpec((None, tm, D_RNN), lambda i: (0, i, 0)),
            pl.BlockSpec((None, tm, D_RNN), lambda i: (1, i, 0)),
            pl.BlockSpec((tm, D_RNN), lambda i: (i, 1)),
            pl.BlockSpec((tm, D_MODEL), lambda i: (i, 0)),
            pl.BlockSpec((tm, D_MODEL), lambda i: (i, 1)),
            pl.BlockSpec((ATTN_OUT_W, D_MODEL), const),
            pl.BlockSpec((D_RNN, D_MODEL), const),
            pl.BlockSpec((D_MODEL, D_MODEL), const),
        ],
        out_specs=pl.BlockSpec((tm, D_MODEL), lambda i: (i, 0)),
        compiler_params=_params("parallel"),
        name="merge_out_proj",
    )(x, ya, h, h, xg, gab, gab, wa, wb, wo)


def _alibi_slopes():
    h = jnp.arange(1, N_ATTN_HEADS + 1, dtype=F32)
    return (2.0 ** (-8.0 * h / N_ATTN_HEADS)).reshape(N_GROUPS, HEADS_PER_GROUP)


def _prepare_layer(p, l):
    qkv_w = 3 * ATTN_QKV_W
    xg_w = qkv_w + 2 * D_RNN
    w_in = p["w_in"][l].astype(BF16)
    b_in = p["b_in"][l][None, :]
    w_gates = jnp.concatenate([p["rg_w_a"][l], p["rg_w_x"][l]], axis=-1).astype(BF16)
    row = lambda v: v[None, :]
    return dict(
        ffn1=(row(p["ffn1_norm"][l]), p["ffn1_w_gate"][l].astype(BF16), p["ffn1_w_up"][l].astype(BF16),
              p["ffn1_w_down"][l].astype(BF16)),
        ffn2=(row(p["ffn2_norm"][l]), p["ffn2_w_gate"][l].astype(BF16), p["ffn2_w_up"][l].astype(BF16),
              p["ffn2_w_down"][l].astype(BF16)),
        mix_norm=row(p["mix_norm"][l]),
        w_qkv=w_in[:, :qkv_w], b_qkv=b_in[:, :qkv_w],
        w_xg=w_in[:, qkv_w:xg_w], b_xg=b_in[:, qkv_w:xg_w],
        w_gab=w_in[:, xg_w:], b_gab=b_in[:, xg_w:],
        conv_w=p["conv_w"][l], conv_b=row(p["conv_b"][l]),
        w_gates=w_gates,
        b_a=p["rg_b_a"][l][:, None, :], b_x=p["rg_b_x"][l][:, None, :], lam=p["rg_lambda"][l][:, None, :],
        w_proj_a=p["w_proj_a"][l].astype(BF16), w_proj_b=p["w_proj_b"][l].astype(BF16),
        w_out=p["w_out"][l].astype(BF16),
    )


def _trunk(x, layers, final_g, slopes):
    batch, seq, _ = x.shape
    m = batch * seq
    x = x.reshape(m, D_MODEL)
    for l, w in enumerate(layers):
        x = _ffn(x, *w["ffn1"], final_g, final_norm=False)
        qkv = _norm_proj(x, w["mix_norm"], w["w_qkv"], w["b_qkv"], tn=ATTN_QKV_W)
        xg = _norm_proj(x, w["mix_norm"], w["w_xg"], w["b_xg"], tn=D_RNN)
        gab = _norm_proj(x, w["mix_norm"], w["w_gab"], w["b_gab"], tn=D_MODEL)
        ya = _attention(qkv.reshape(batch, seq, 3 * ATTN_QKV_W), slopes, batch=batch, seq=seq)
        h = _rglru(xg.reshape(batch, seq, 2 * D_RNN), w["conv_w"], w["conv_b"], w["w_gates"],
                   w["b_a"], w["b_x"], w["lam"], batch=batch, seq=seq)
        x = _merge(x, ya.reshape(m, ATTN_OUT_W), h.reshape(2, m, D_RNN), xg, gab,
                   w["w_proj_a"], w["w_proj_b"], w["w_out"])
        x = _ffn(x, *w["ffn2"], final_g, final_norm=(l == len(layers) - 1))
    return x.reshape(batch, seq, D_MODEL)


def kernel(x_prompt, x_sample, ffn1_norm, ffn1_w_gate, ffn1_w_up, ffn1_w_down, mix_norm, w_in, b_in, conv_w, conv_b, rg_w_a, rg_b_a, rg_w_x, rg_b_x, rg_lambda, w_proj_a, w_proj_b, w_out, ffn2_norm, ffn2_w_gate, ffn2_w_up, ffn2_w_down, final_norm):
    p = dict(ffn1_norm=ffn1_norm, ffn1_w_gate=ffn1_w_gate, ffn1_w_up=ffn1_w_up, ffn1_w_down=ffn1_w_down,
             mix_norm=mix_norm, w_in=w_in, b_in=b_in, conv_w=conv_w, conv_b=conv_b,
             rg_w_a=rg_w_a, rg_b_a=rg_b_a, rg_w_x=rg_w_x, rg_b_x=rg_b_x, rg_lambda=rg_lambda,
             w_proj_a=w_proj_a, w_proj_b=w_proj_b, w_out=w_out,
             ffn2_norm=ffn2_norm, ffn2_w_gate=ffn2_w_gate, ffn2_w_up=ffn2_w_up, ffn2_w_down=ffn2_w_down)
    layers = [_prepare_layer(p, l) for l in range(w_in.shape[0])]
    final_g = final_norm[None, :]
    slopes = _alibi_slopes()
    return (_trunk(x_prompt, layers, final_g, slopes), _trunk(x_sample, layers, final_g, slopes))
```

```python
import functools

import jax
import jax.numpy as jnp
from jax import lax
from jax.experimental import pallas as pl
from jax.experimental.pallas import tpu as pltpu

F32 = jnp.float32
BF16 = jnp.bfloat16

D_MODEL = 1024
DEPTH = 4
ATTN_GROUPS = ((128, 1), (512, 4), (2048, 16))
N_GROUPS = 3
HEADS_PER_GROUP = 4
HEAD_DIM = 128
N_ATTN_HEADS = N_GROUPS * HEADS_PER_GROUP
ATTN_QKV_W = N_ATTN_HEADS * HEAD_DIM
ATTN_OUT_W = HEADS_PER_GROUP * HEAD_DIM
D_RNN = 1280
RG_BLOCK = 128
RG_BLOCKS = D_RNN // RG_BLOCK
RG_C = 8.0
D_FF = 2816
NORM_EPS = 1e-6
MASK_VALUE = -1e30
N_SIDE = 64

VMEM_LIMIT_BYTES = 56 * 1024 * 1024
SUBLANES = 8
LANES = 128


def _params(*semantics):
    return pltpu.CompilerParams(dimension_semantics=semantics, vmem_limit_bytes=VMEM_LIMIT_BYTES)


def _rmsnorm_rows(x, g):
    y = x * lax.rsqrt(jnp.mean(x * x, axis=-1, keepdims=True) + NORM_EPS)
    return y * g


def _ffn_kernel(x_ref, g_ref, wg_ref, wu_ref, wd_ref, fg_ref, o_ref, xn_ref, acc_ref, *, n_ff, final_norm):
    j = pl.program_id(1)

    @pl.when(j == 0)
    def _():
        xn_ref[...] = _rmsnorm_rows(x_ref[...], g_ref[...]).astype(BF16)

    xn = xn_ref[...]
    gate = jnp.dot(xn, wg_ref[...], preferred_element_type=F32)
    up = jnp.dot(xn, wu_ref[...], preferred_element_type=F32)
    hid = (gate * jax.nn.sigmoid(gate) * up).astype(BF16)
    down = jnp.dot(hid, wd_ref[...], preferred_element_type=F32)

    @pl.when(j == 0)
    def _():
        acc_ref[...] = down

    @pl.when(j > 0)
    def _():
        acc_ref[...] += down

    @pl.when(j == n_ff - 1)
    def _():
        y = x_ref[...] + 0.5 * acc_ref[...]
        if final_norm:
            y = _rmsnorm_rows(y, fg_ref[...])
        o_ref[...] = y


def _ffn(x, g, wg, wu, wd, fg, *, final_norm, tm=512, tf=1408):
    m = x.shape[0]
    n_ff = D_FF // tf
    return pl.pallas_call(
        functools.partial(_ffn_kernel, n_ff=n_ff, final_norm=final_norm),
        out_shape=jax.ShapeDtypeStruct((m, D_MODEL), F32),
        grid=(m // tm, n_ff),
        in_specs=[
            pl.BlockSpec((tm, D_MODEL), lambda i, j: (i, 0)),
            pl.BlockSpec((1, D_MODEL), lambda i, j: (0, 0)),
            pl.BlockSpec((D_MODEL, tf), lambda i, j: (0, j)),
            pl.BlockSpec((D_MODEL, tf), lambda i, j: (0, j)),
            pl.BlockSpec((tf, D_MODEL), lambda i, j: (j, 0)),
            pl.BlockSpec((1, D_MODEL), lambda i, j: (0, 0)),
        ],
        out_specs=pl.BlockSpec((tm, D_MODEL), lambda i, j: (i, 0)),
        scratch_shapes=[pltpu.VMEM((tm, D_MODEL), BF16), pltpu.VMEM((tm, D_MODEL), F32)],
        compiler_params=_params("parallel", "arbitrary"),
        name="ffn",
    )(x, g, wg, wu, wd, fg)


def _norm_proj_kernel(x_ref, g_ref, w_ref, b_ref, o_ref, xn_ref):
    @pl.when(pl.program_id(1) == 0)
    def _():
        xn_ref[...] = _rmsnorm_rows(x_ref[...], g_ref[...]).astype(BF16)

    o_ref[...] = jnp.dot(xn_ref[...], w_ref[...], preferred_element_type=F32) + b_ref[...]


def _norm_proj(x, g, w, b, *, tn, tm=512):
    m = x.shape[0]
    n = w.shape[1]
    return pl.pallas_call(
        _norm_proj_kernel,
        out_shape=jax.ShapeDtypeStruct((m, n), F32),
        grid=(m // tm, n // tn),
        in_specs=[
            pl.BlockSpec((tm, D_MODEL), lambda i, j: (i, 0)),
            pl.BlockSpec((1, D_MODEL), lambda i, j: (0, 0)),
            pl.BlockSpec((D_MODEL, tn), lambda i, j: (0, j)),
            pl.BlockSpec((1, tn), lambda i, j: (0, j)),
        ],
        out_specs=pl.BlockSpec((tm, tn), lambda i, j: (i, j)),
        scratch_shapes=[pltpu.VMEM((tm, D_MODEL), BF16)],
        compiler_params=_params("parallel", "arbitrary"),
        name="norm_proj",
    )(x, g, w, b)


ATTN_BQ = 128
ATTN_BK = 256
ATTN_INTERLEAVE = 2
ATTN_COPY_ROWS = 128
LOG2_E = 1.4426950408889634
N_BIAS = 3


def _attn_kernel(slopes_ref, q_ref, k_ref, v_ref, o_ref, qs_ref, ks_ref, vs_ref, bias_ref, m_ref, l_ref, acc_ref,
                 *, seq):
    head = pl.program_id(1)
    grp = pl.program_id(2)

    def run_group(g, dilation):
        sub_len = seq // dilation
        bq = min(ATTN_BQ, sub_len)
        bk = min(ATTN_BK, sub_len)
        blocks_per_res = sub_len // bq
        n_blocks = dilation * blocks_per_res
        unroll = ATTN_INTERLEAVE if n_blocks % ATTN_INTERLEAVE == 0 else 1

        slope = slopes_ref[g, head] * (float(dilation) * LOG2_E)
        row = lax.broadcasted_iota(jnp.int32, (bq, bk), 0)
        col = lax.broadcasted_iota(jnp.int32, (bq, bk), 1)
        for var in range(N_BIAS):
            dist = jnp.abs(col - row - var * N_SIDE)
            bias_ref[var, :bq, :bk] = jnp.where(dist <= N_SIDE, -slope * dist.astype(F32), MASK_VALUE)

        cp = min(ATTN_COPY_ROWS, sub_len)
        copies_per_res = sub_len // cp

        def copy(i, carry):
            res = i // copies_per_res
            c0 = (i % copies_per_res) * cp
            src = pl.ds(res + c0 * dilation, cp, stride=dilation)
            dst = pl.ds(pl.multiple_of(res * sub_len + c0, cp), cp)
            qs_ref[dst, :] = (q_ref[src, :] * (HEAD_DIM ** -0.5 * LOG2_E)).astype(BF16)
            ks_ref[dst, :] = k_ref[src, :].astype(BF16)
            vs_ref[dst, :] = v_ref[src, :].astype(BF16)
            return carry

        lax.fori_loop(0, dilation * copies_per_res, copy, 0)

        def blocks(it, carry):
            loaded = []
            for u in range(unroll):
                i = it * unroll + u
                res = i // blocks_per_res
                m0 = (i % blocks_per_res) * bq
                k0 = jnp.clip(m0 - N_SIDE, 0, sub_len - bk)
                base = res * sub_len
                q = qs_ref[pl.ds(pl.multiple_of(base + m0, bq), bq), :]
                k = ks_ref[pl.ds(pl.multiple_of(base + k0, N_SIDE), bk), :]
                v = vs_ref[pl.ds(pl.multiple_of(base + k0, N_SIDE), bk), :]
                bias = bias_ref[(m0 - k0) // N_SIDE, :bq, :bk]
                rows = pl.ds(res + m0 * dilation, bq, stride=dilation)
                old = None if g == 0 else (m_ref[rows, :], l_ref[rows, :], acc_ref[rows, :])
                loaded.append((q, k, v, bias, rows, old))

            results = []
            for q, k, v, bias, rows, old in loaded:
                s = lax.dot_general(q, k, (((1,), (1,)), ((), ())), preferred_element_type=F32) + bias
                m_col = jnp.max(s, axis=-1, keepdims=True)
                if old is not None:
                    m_col = jnp.maximum(old[0][:, :1], m_col)
                p = jnp.exp2(s - m_col).astype(BF16)
                v_ones = jnp.concatenate([v, jnp.ones_like(v)], axis=1)
                pv = jnp.dot(p, v_ones, preferred_element_type=F32)
                m_new = jnp.broadcast_to(m_col, (bq, HEAD_DIM))
                acc_new, l_new = pv[:, :HEAD_DIM], pv[:, HEAD_DIM:]
                if old is not None:
                    alpha = jnp.exp2(old[0] - m_new)
                    l_new = alpha * old[1] + l_new
                    acc_new = alpha * old[2] + acc_new
                results.append((rows, m_new, l_new, acc_new))

            for rows, m_new, l_new, acc_new in results:
                m_ref[rows, :] = m_new
                l_ref[rows, :] = l_new
                acc_ref[rows, :] = acc_new
            return carry

        lax.fori_loop(0, n_blocks // unroll, blocks, 0)

    for g, (_, dilation) in enumerate(ATTN_GROUPS):
        pl.when(grp == g)(functools.partial(run_group, g, dilation))

    @pl.when(grp == N_GROUPS - 1)
    def _():
        def finish(i, carry):
            rows = pl.ds(pl.multiple_of(i * ATTN_BQ, ATTN_BQ), ATTN_BQ)
            o_ref[rows, :] = (acc_ref[rows, :] / l_ref[rows, :]).astype(o_ref.dtype)
            return carry

        lax.fori_loop(0, seq // ATTN_BQ, finish, 0)


def _attention(qkv, slopes, *, batch, seq):
    def col(part):
        return lambda b, h, g: (b, 0, part * N_ATTN_HEADS + g * HEADS_PER_GROUP + h)

    return pl.pallas_call(
        functools.partial(_attn_kernel, seq=seq),
        out_shape=jax.ShapeDtypeStruct((batch, seq, ATTN_OUT_W), BF16),
        grid=(batch, HEADS_PER_GROUP, N_GROUPS),
        in_specs=[
            pl.BlockSpec(memory_space=pltpu.SMEM),
            pl.BlockSpec((None, seq, HEAD_DIM), col(0)),
            pl.BlockSpec((None, seq, HEAD_DIM), col(1)),
            pl.BlockSpec((None, seq, HEAD_DIM), col(2)),
        ],
        out_specs=pl.BlockSpec((None, seq, HEAD_DIM), lambda b, h, g: (b, 0, h)),
        scratch_shapes=[
            pltpu.VMEM((seq, HEAD_DIM), BF16),
            pltpu.VMEM((seq, HEAD_DIM), BF16),
            pltpu.VMEM((seq, HEAD_DIM), BF16),
            pltpu.VMEM((N_BIAS, ATTN_BQ, ATTN_BK), F32),
            pltpu.VMEM((seq, HEAD_DIM), F32),
            pltpu.VMEM((seq, HEAD_DIM), F32),
            pltpu.VMEM((seq, HEAD_DIM), F32),
        ],
        compiler_params=_params("parallel", "parallel", "arbitrary"),
        name="dilated_attention",
    )(slopes, qkv, qkv, qkv)


RG_TS = 32
CONV_LEFT = 2


def _rglru_kernel(xr_ref, prev_ref, next_ref, cw_ref, cb_ref, wg_ref, ba_ref, bx_ref, lam_ref,
                  h_ref, xt_ref, a_ref, u_ref, carry_ref, *, batch, n_chunks):
    ts = RG_TS
    direction = pl.program_id(0)
    step = pl.program_id(1)
    chunk = jnp.where(direction == 0, step, n_chunks - 1 - step)
    rows = ts * batch
    has_prev = chunk > 0
    has_next = chunk < n_chunks - 1

    @pl.when(step == 0)
    def _():
        carry_ref[...] = jnp.zeros_like(carry_ref)

    for n in range(RG_BLOCKS):
        cols = slice(n * RG_BLOCK, (n + 1) * RG_BLOCK)
        for b in range(batch):
            xt_ref[n, pl.ds(CONV_LEFT * batch + b, ts, stride=batch), :] = xr_ref[b, :, cols]
            halo = prev_ref[b, :, cols]
            for t in range(CONV_LEFT):
                r = t * batch + b
                src = SUBLANES - CONV_LEFT + t
                xt_ref[n, r:r + 1, :] = jnp.where(has_prev, halo[src:src + 1, :], 0.0)
            r = (CONV_LEFT + ts) * batch + b
            xt_ref[n, r:r + 1, :] = jnp.where(has_next, next_ref[b, 0:1, cols], 0.0)

        xc = cb_ref[:, cols]
        for tap in range(4):
            xc = xc + cw_ref[tap:tap + 1, cols] * xt_ref[n, tap * batch:tap * batch + rows, :]
        gates = jnp.dot(xc.astype(BF16), wg_ref[n], preferred_element_type=F32)
        r_gate = jax.nn.sigmoid(gates[:, :RG_BLOCK] + ba_ref[:, cols])
        i_gate = jax.nn.sigmoid(gates[:, RG_BLOCK:] + bx_ref[:, cols])
        log_a = (-RG_C) * r_gate * jax.nn.softplus(-lam_ref[:, cols])
        a = jnp.exp(log_a)
        a_ref[n] = a
        u_ref[n] = jnp.sqrt(-jnp.tanh(log_a) * (a * a + 1.0)) * (i_gate * xc)

    def scan_step(t, h):
        tt = jnp.where(direction == 0, t, ts - 1 - t)
        r = pl.ds(pl.multiple_of(tt * batch, batch), batch)
        h = a_ref[:, r, :] * h + u_ref[:, r, :]
        u_ref[:, r, :] = h
        return h

    carry_ref[...] = lax.fori_loop(0, ts, scan_step, carry_ref[...], unroll=4)

    for n in range(RG_BLOCKS):
        for b in range(batch):
            h_ref[b, :, n * RG_BLOCK:(n + 1) * RG_BLOCK] = u_ref[n, pl.ds(b, ts, stride=batch), :]


def _rglru(xg, conv_w, conv_b, w_gates, b_a, b_x, lam, *, batch, seq):
    ts = RG_TS
    n_chunks = seq // ts
    halo_per_chunk = ts // SUBLANES
    n_halo = seq // SUBLANES

    def chunk_of(d, s):
        return s + d * (n_chunks - 1 - 2 * s)

    return pl.pallas_call(
        functools.partial(_rglru_kernel, batch=batch, n_chunks=n_chunks),
        out_shape=jax.ShapeDtypeStruct((2, batch, seq, D_RNN), F32),
        grid=(2, n_chunks),
        in_specs=[
            pl.BlockSpec((batch, ts, D_RNN), lambda d, s: (0, chunk_of(d, s), 0)),
            pl.BlockSpec((batch, SUBLANES, D_RNN),
                         lambda d, s: (0, jnp.maximum(chunk_of(d, s) * halo_per_chunk - 1, 0), 0)),
            pl.BlockSpec((batch, SUBLANES, D_RNN),
                         lambda d, s: (0, jnp.minimum((chunk_of(d, s) + 1) * halo_per_chunk, n_halo - 1), 0)),
            pl.BlockSpec((4, D_RNN), lambda d, s: (0, 0)),
            pl.BlockSpec((1, D_RNN), lambda d, s: (0, 0)),
            pl.BlockSpec((None, RG_BLOCKS, RG_BLOCK, 2 * RG_BLOCK), lambda d, s: (d, 0, 0, 0)),
            pl.BlockSpec((None, 1, D_RNN), lambda d, s: (d, 0, 0)),
            pl.BlockSpec((None, 1, D_RNN), lambda d, s: (d, 0, 0)),
            pl.BlockSpec((None, 1, D_RNN), lambda d, s: (d, 0, 0)),
        ],
        out_specs=pl.BlockSpec((None, batch, ts, D_RNN), lambda d, s: (d, 0, chunk_of(d, s), 0)),
        scratch_shapes=[
            pltpu.VMEM((RG_BLOCKS, (ts + CONV_LEFT + 1) * batch, RG_BLOCK), F32),
            pltpu.VMEM((RG_BLOCKS, ts * batch, RG_BLOCK), F32),
            pltpu.VMEM((RG_BLOCKS, ts * batch, RG_BLOCK), F32),
            pltpu.VMEM((RG_BLOCKS, batch, RG_BLOCK), F32),
        ],
        compiler_params=_params("arbitrary", "arbitrary"),
        name="rglru_scan",
    )(xg, xg, xg, conv_w, conv_b, w_gates, b_a, b_x, lam)


def _merge_kernel(x_ref, ya_ref, hf_ref, hb_ref, gr_ref, ga_ref, gb_ref, wa_ref, wb_ref, wo_ref, o_ref):
    yb = ((hf_ref[...] + hb_ref[...]) * jax.nn.gelu(gr_ref[...])).astype(BF16)
    pa = jnp.dot(ya_ref[...], wa_ref[...], preferred_element_type=F32)
    pb = jnp.dot(yb, wb_ref[...], preferred_element_type=F32)
    merged = jax.nn.sigmoid(ga_ref[...]) * pa + jax.nn.sigmoid(gb_ref[...]) * pb
    o_ref[...] = x_ref[...] + jnp.dot(merged.astype(BF16), wo_ref[...], preferred_element_type=F32)


def _merge(x, ya, h, xg, gab, wa, wb, wo, *, tm=256):
    m = x.shape[0]
    const = lambda i: (0, 0)
    return pl.pallas_call(
        _merge_kernel,
        out_shape=jax.ShapeDtypeStruct((m, D_MODEL), F32),
        grid=(m // tm,),
        in_specs=[
            pl.BlockSpec((tm, D_MODEL), lambda i: (i, 0)),
            pl.BlockSpec((tm, ATTN_OUT_W), lambda i: (i, 0)),
            pl.BlockSpec((None, tm, D_RNN), lambda i: (0, i, 0)),
            pl.BlockSpec((None, tm, D_RNN), lambda i: (1, i, 0)),
            pl.BlockSpec((tm, D_RNN), lambda i: (i, 1)),
            pl.BlockSpec((tm, D_MODEL), lambda i: (i, 0)),
            pl.BlockSpec((tm, D_MODEL), lambda i: (i, 1)),
            pl.BlockSpec((ATTN_OUT_W, D_MODEL), const),
            pl.BlockSpec((D_RNN, D_MODEL), const),
            pl.BlockSpec((D_MODEL, D_MODEL), const),
        ],
        out_specs=pl.BlockSpec((tm, D_MODEL), lambda i: (i, 0)),
        compiler_params=_params("parallel"),
        name="merge_out_proj",
    )(x, ya, h, h, xg, gab, gab, wa, wb, wo)


def _alibi_slopes():
    h = jnp.arange(1, N_ATTN_HEADS + 1, dtype=F32)
    return (2.0 ** (-8.0 * h / N_ATTN_HEADS)).reshape(N_GROUPS, HEADS_PER_GROUP)


def _prepare_layer(p, l):
    qkv_w = 3 * ATTN_QKV_W
    xg_w = qkv_w + 2 * D_RNN
    w_in = p["w_in"][l].astype(BF16)
    b_in = p["b_in"][l][None, :]
    w_gates = jnp.concatenate([p["rg_w_a"][l], p["rg_w_x"][l]], axis=-1).astype(BF16)
    row = lambda v: v[None, :]
    return dict(
        ffn1=(row(p["ffn1_norm"][l]), p["ffn1_w_gate"][l].astype(BF16), p["ffn1_w_up"][l].astype(BF16),
              p["ffn1_w_down"][l].astype(BF16)),
        ffn2=(row(p["ffn2_norm"][l]), p["ffn2_w_gate"][l].astype(BF16), p["ffn2_w_up"][l].astype(BF16),
              p["ffn2_w_down"][l].astype(BF16)),
        mix_norm=row(p["mix_norm"][l]),
        w_qkv=w_in[:, :qkv_w], b_qkv=b_in[:, :qkv_w],
        w_xg=w_in[:, qkv_w:xg_w], b_xg=b_in[:, qkv_w:xg_w],
        w_gab=w_in[:, xg_w:], b_gab=b_in[:, xg_w:],
        conv_w=p["conv_w"][l], conv_b=row(p["conv_b"][l]),
        w_gates=w_gates,
        b_a=p["rg_b_a"][l][:, None, :], b_x=p["rg_b_x"][l][:, None, :], lam=p["rg_lambda"][l][:, None, :],
        w_proj_a=p["w_proj_a"][l].astype(BF16), w_proj_b=p["w_proj_b"][l].astype(BF16),
        w_out=p["w_out"][l].astype(BF16),
    )


def _trunk(x, layers, final_g, slopes):
    batch, seq, _ = x.shape
    m = batch * seq
    x = x.reshape(m, D_MODEL)
    for l, w in enumerate(layers):
        x = _ffn(x, *w["ffn1"], final_g, final_norm=False)
        qkv = _norm_proj(x, w["mix_norm"], w["w_qkv"], w["b_qkv"], tn=ATTN_QKV_W)
        xg = _norm_proj(x, w["mix_norm"], w["w_xg"], w["b_xg"], tn=D_RNN)
        gab = _norm_proj(x, w["mix_norm"], w["w_gab"], w["b_gab"], tn=D_MODEL)
        ya = _attention(qkv.reshape(batch, seq, 3 * ATTN_QKV_W), slopes, batch=batch, seq=seq)
        h = _rglru(xg.reshape(batch, seq, 2 * D_RNN), w["conv_w"], w["conv_b"], w["w_gates"],
                   w["b_a"], w["b_x"], w["lam"], batch=batch, seq=seq)
        x = _merge(x, ya.reshape(m, ATTN_OUT_W), h.reshape(2, m, D_RNN), xg, gab,
                   w["w_proj_a"], w["w_proj_b"], w["w_out"])
        x = _ffn(x, *w["ffn2"], final_g, final_norm=(l == len(layers) - 1))
    return x.reshape(batch, seq, D_MODEL)


def kernel(x_prompt, x_sample, ffn1_norm, ffn1_w_gate, ffn1_w_up, ffn1_w_down, mix_norm, w_in, b_in, conv_w, conv_b, rg_w_a, rg_b_a, rg_w_x, rg_b_x, rg_lambda, w_proj_a, w_proj_b, w_out, ffn2_norm, ffn2_w_gate, ffn2_w_up, ffn2_w_down, final_norm):
    p = dict(ffn1_norm=ffn1_norm, ffn1_w_gate=ffn1_w_gate, ffn1_w_up=ffn1_w_up, ffn1_w_down=ffn1_w_down,
             mix_norm=mix_norm, w_in=w_in, b_in=b_in, conv_w=conv_w, conv_b=conv_b,
             rg_w_a=rg_w_a, rg_b_a=rg_b_a, rg_w_x=rg_w_x, rg_b_x=rg_b_x, rg_lambda=rg_lambda,
             w_proj_a=w_proj_a, w_proj_b=w_proj_b, w_out=w_out,
             ffn2_norm=ffn2_norm, ffn2_w_gate=ffn2_w_gate, ffn2_w_up=ffn2_w_up, ffn2_w_down=ffn2_w_down)
    layers = [_prepare_layer(p, l) for l in range(w_in.shape[0])]
    final_g = final_norm[None, :]
    slopes = _alibi_slopes()
    return (_trunk(x_prompt, layers, final_g, slopes), _trunk(x_sample, layers, final_g, slopes))
```
